```python
import jax, jax.numpy as jnp
from jax import lax
import numpy as np

D_MODEL = 1024
BATCH = 4
SEQ = 8192
DEPTH = 1
DEC_BATCH = 8
DEC_SEQ = 32
PAST_LEN = 4096

CHUNK = 64
N_META = 16
Q_BLOCK = 128
META_PAD = Q_BLOCK - N_META
RET_HEADS = 4
RET_QK_DIM = D_MODEL // 8
RET_V_DIM = D_MODEL // 4
RET_QK_W = RET_HEADS * RET_QK_DIM
RET_V_W = RET_HEADS * RET_V_DIM
FOX_HEADS = 8
FOX_DIM = D_MODEL // 8
FOX_W = FOX_HEADS * FOX_DIM
D_FF = 4 * D_MODEL
ROPE_BASE = 10000.0
EPS = 1e-6
NEG_INF = -1e30
IN_SIZES = (RET_QK_W, RET_QK_W, RET_V_W, RET_V_W, FOX_W, FOX_W, FOX_W, FOX_HEADS, D_MODEL, D_MODEL)
IN_W = 2 * RET_QK_W + 2 * RET_V_W + 3 * FOX_W + FOX_HEADS + 2 * D_MODEL

kernel_name = "hybrid_retention_fox_stream_step"


def rmsnorm(x, g):
    xf = x.astype(jnp.float32)
    y = xf * lax.rsqrt(jnp.mean(xf * xf, axis=-1, keepdims=True) + EPS)
    return (y * g.astype(jnp.float32)).astype(x.dtype)


def rotary(x, pos):
    half = x.shape[-1] // 2
    inv = ROPE_BASE ** (-jnp.arange(half, dtype=jnp.float32) / half)
    ang = pos.astype(jnp.float32)[:, None] * inv[None, :]
    cos = jnp.cos(ang)[:, None, :]
    sin = jnp.sin(ang)[:, None, :]
    xf = x.astype(jnp.float32)
    x1, x2 = xf[..., :half], xf[..., half:]
    return jnp.concatenate([x1 * cos - x2 * sin, x2 * cos + x1 * sin], axis=-1).astype(x.dtype)


def retention_log_decay():
    return jnp.log(1.0 - 2.0 ** (-5.0 - jnp.arange(RET_HEADS, dtype=jnp.float32)))


def retention_block(q, k, v, state, log_gamma):
    c_len = q.shape[2]
    n = jnp.arange(c_len, dtype=jnp.float32)
    lg = log_gamma[:, None]
    diff = n[:, None] - n[None, :]
    decay = jnp.where(diff >= 0, jnp.exp(lg[..., None] * jnp.maximum(diff, 0.0)), 0.0)
    qf, kf, vf = q.astype(jnp.float32), k.astype(jnp.float32), v.astype(jnp.float32)
    scores = jnp.einsum('bhnd,bhmd->bhnm', qf, kf) * decay
    inner = jnp.einsum('bhnm,bhmv->bhnv', scores, vf)
    cross = jnp.einsum('bhnd,bhdv->bhnv', qf * jnp.exp(lg * (n + 1.0))[..., None], state)
    new_state = jnp.exp(lg * c_len)[..., None] * state + jnp.einsum(
        'bhmd,bhmv->bhdv', kf * jnp.exp(lg * (c_len - 1.0 - n))[..., None], vf)
    return inner + cross, new_state


def retention_out(y, rg, gn_g):
    y = jnp.swapaxes(y, 1, 2)
    b, t = y.shape[0], y.shape[1]
    yc = y - jnp.mean(y, axis=-1, keepdims=True)
    yn = yc * lax.rsqrt(jnp.mean(yc * yc, axis=-1, keepdims=True) + EPS)
    yn = yn.reshape(b, t, RET_V_W) * gn_g.astype(jnp.float32)
    return (jax.nn.silu(rg.astype(jnp.float32)) * yn).astype(rg.dtype)


def fox_attend(q, k, v, cq, ck, allowed):
    s = jnp.einsum('bhqd,bhkd->bhqk', q.astype(jnp.float32), k.astype(jnp.float32)) * (FOX_DIM ** -0.5)
    s = s + cq[..., :, None] - ck[..., None, :]
    s = jnp.where(allowed, s, NEG_INF)
    p = jax.nn.softmax(s, axis=-1)
    return jnp.einsum('bhqk,bhkd->bhqd', p, v.astype(jnp.float32))


def project_in(h, pos, norm1_g, w_in, b_forget, q_norm_g, k_norm_g):
    b, t, _ = h.shape
    xn = rmsnorm(h, norm1_g)
    z = jnp.einsum('btd,de->bte', xn, w_in)
    cuts = [int(c) for c in np.cumsum(IN_SIZES)[:-1]]
    rq, rk, rv, rg, fq, fk, fv, ff, ga, gb = jnp.split(z, cuts, axis=-1)
    rq = rotary(rq.reshape(b, t, RET_HEADS, RET_QK_DIM), pos)
    rk = rotary(rk.reshape(b, t, RET_HEADS, RET_QK_DIM), pos) * (RET_QK_DIM ** -0.5)
    rv = rv.reshape(b, t, RET_HEADS, RET_V_DIM)
    fq = rmsnorm(fq.reshape(b, t, FOX_HEADS, FOX_DIM), q_norm_g)
    fk = rmsnorm(fk.reshape(b, t, FOX_HEADS, FOX_DIM), k_norm_g)
    fv = fv.reshape(b, t, FOX_HEADS, FOX_DIM)
    logf = jax.nn.log_sigmoid(ff.astype(jnp.float32) + b_forget.astype(jnp.float32))
    return rq, rk, rv, rg, fq, fk, fv, logf, ga, gb


def merge_and_mlp(h, ret_o, fox_o, ga, gb, w_ret_out, w_fox_out, w_o, norm2_g, w_ff1, w_ff2):
    m = jax.nn.sigmoid(ga) * (ret_o @ w_ret_out) + jax.nn.sigmoid(gb) * (fox_o @ w_fox_out)
    h = h + m @ w_o
    u = jax.nn.relu(rmsnorm(h, norm2_g) @ w_ff1)
    return h + (u * u) @ w_ff2


def prompt_layer(h, pos, valid, lw):
    (norm1_g, w_in, b_forget, q_norm_g, k_norm_g, ret_gn_g,
     w_ret_out, w_fox_out, w_o, norm2_g, w_ff1, w_ff2) = lw
    b, l_len, _ = h.shape
    rq, rk, rv, rg, fq, fk, fv, logf, ga, gb = project_in(h, pos, norm1_g, w_in, b_forget, q_norm_g, k_norm_g)
    rk = rk * valid[None, :, None, None].astype(rk.dtype)
    n_chunks = l_len // CHUNK
    log_gamma = retention_log_decay()

    def to_chunks(a):
        return a.reshape(b, n_chunks, CHUNK, a.shape[2], a.shape[3]).transpose(1, 0, 3, 2, 4)

    def step(state, qkv):
        q_c, k_c, v_c = qkv
        o_c, state = retention_block(q_c, k_c, v_c, state, log_gamma)
        return state, o_c

    s0 = jnp.zeros((b, RET_HEADS, RET_QK_DIM, RET_V_DIM), jnp.float32)
    s_fin, o = lax.scan(step, s0, (to_chunks(rq), to_chunks(rk), to_chunks(rv)))
    o = o.transpose(1, 2, 0, 3, 4).reshape(b, RET_HEADS, l_len, RET_V_DIM)
    ret_o = retention_out(o, rg, ret_gn_g)
    logf = jnp.where(valid[None, :, None], logf, 0.0)
    c = jnp.cumsum(logf, axis=1).transpose(0, 2, 1)
    qh, kh, vh = fq.transpose(0, 2, 1, 3), fk.transpose(0, 2, 1, 3), fv.transpose(0, 2, 1, 3)
    n_blocks = l_len // Q_BLOCK
    qb = qh.reshape(b, FOX_HEADS, n_blocks, Q_BLOCK, FOX_DIM).transpose(2, 0, 1, 3, 4)
    cb = c.reshape(b, FOX_HEADS, n_blocks, Q_BLOCK).transpose(2, 0, 1, 3)
    kpos = jnp.arange(l_len)

    def attend_block(args):
        q_i, c_i, blk = args
        qpos = blk * Q_BLOCK + jnp.arange(Q_BLOCK)
        allowed = (kpos[None, :] <= qpos[:, None]) & valid[None, :]
        return fox_attend(q_i, kh, vh, c_i, c, allowed)

    ob = lax.map(attend_block, (qb, cb, jnp.arange(n_blocks)))
    fox_o = ob.transpose(1, 0, 3, 2, 4).reshape(b, l_len, FOX_W).astype(h.dtype)
    h = merge_and_mlp(h, ret_o, fox_o, ga, gb, w_ret_out, w_fox_out, w_o, norm2_g, w_ff1, w_ff2)
    return h, s_fin, fk[:, META_PAD:], fv[:, META_PAD:], logf[:, META_PAD:]


def sample_layer(h, state, cache_k, cache_v, cache_logf, lw):
    (norm1_g, w_in, b_forget, q_norm_g, k_norm_g, ret_gn_g,
     w_ret_out, w_fox_out, w_o, norm2_g, w_ff1, w_ff2) = lw
    b, t, _ = h.shape
    past = cache_k.shape[1]
    pos = past + jnp.arange(t)
    rq, rk, rv, rg, fq, fk, fv, logf, ga, gb = project_in(h, pos, norm1_g, w_in, b_forget, q_norm_g, k_norm_g)
    o, s_new = retention_block(rq.transpose(0, 2, 1, 3), rk.transpose(0, 2, 1, 3), rv.transpose(0, 2, 1, 3),
                               state.astype(jnp.float32), retention_log_decay())
    ret_o = retention_out(o, rg, ret_gn_g)
    k_all = jnp.concatenate([cache_k.astype(fk.dtype), fk], axis=1).transpose(0, 2, 1, 3)
    v_all = jnp.concatenate([cache_v.astype(fv.dtype), fv], axis=1).transpose(0, 2, 1, 3)
    c = jnp.cumsum(jnp.concatenate([cache_logf.astype(jnp.float32), logf], axis=1), axis=1).transpose(0, 2, 1)
    allowed = jnp.arange(past + t)[None, :] <= (past + jnp.arange(t))[:, None]
    fo = fox_attend(fq.transpose(0, 2, 1, 3), k_all, v_all, c[..., past:], c, allowed)
    fox_o = fo.transpose(0, 2, 1, 3).reshape(b, t, FOX_W).astype(h.dtype)
    h = merge_and_mlp(h, ret_o, fox_o, ga, gb, w_ret_out, w_fox_out, w_o, norm2_g, w_ff1, w_ff2)
    return h, s_new.astype(state.dtype), fk, fv, logf


def setup_inputs(seed: int = 0) -> dict:
    key = jax.random.key(seed)
    ks = jax.random.split(key, 20)
    nrm = jax.random.normal
    x_prompt = nrm(ks[0], (BATCH, SEQ, D_MODEL), jnp.float32)
    x_sample = nrm(ks[1], (DEC_BATCH, DEC_SEQ, D_MODEL), jnp.float32)
    state_ret = 0.3 * nrm(ks[2], (DEPTH, DEC_BATCH, RET_HEADS, RET_QK_DIM, RET_V_DIM), jnp.float32)
    cache_fox_k = nrm(ks[3], (DEPTH, DEC_BATCH, PAST_LEN, FOX_HEADS, FOX_DIM), jnp.float32)
    cache_fox_v = nrm(ks[4], (DEPTH, DEC_BATCH, PAST_LEN, FOX_HEADS, FOX_DIM), jnp.float32)
    b_forget = jnp.linspace(1.0, 5.0, FOX_HEADS, dtype=jnp.float32)[None, :] + 0.1 * nrm(ks[5], (DEPTH, FOX_HEADS), jnp.float32)
    cache_fox_logf = jax.nn.log_sigmoid(
        b_forget[:, None, None, :] + 0.5 * nrm(ks[6], (DEPTH, DEC_BATCH, PAST_LEN, FOX_HEADS), jnp.float32))
    meta_tokens = nrm(ks[7], (N_META, D_MODEL), jnp.float32)
    norm1_g = 1.0 + 0.02 * nrm(ks[8], (DEPTH, D_MODEL), jnp.float32)
    w_in = nrm(ks[9], (DEPTH, D_MODEL, IN_W), jnp.float32) * D_MODEL ** -0.5
    q_norm_g = 1.0 + 0.02 * nrm(ks[10], (DEPTH, FOX_DIM), jnp.float32)
    k_norm_g = 1.0 + 0.02 * nrm(ks[11], (DEPTH, FOX_DIM), jnp.float32)
    ret_gn_g = 1.0 + 0.02 * nrm(ks[12], (DEPTH, RET_V_W), jnp.float32)
    w_ret_out = nrm(ks[13], (DEPTH, RET_V_W, D_MODEL), jnp.float32) * RET_V_W ** -0.5
    w_fox_out = nrm(ks[14], (DEPTH, FOX_W, D_MODEL), jnp.float32) * FOX_W ** -0.5
    w_o = nrm(ks[15], (DEPTH, D_MODEL, D_MODEL), jnp.float32) * D_MODEL ** -0.5
    norm2_g = 1.0 + 0.02 * nrm(ks[16], (DEPTH, D_MODEL), jnp.float32)
    w_ff1 = nrm(ks[17], (DEPTH, D_MODEL, D_FF), jnp.float32) * D_MODEL ** -0.5
    w_ff2 = nrm(ks[18], (DEPTH, D_FF, D_MODEL), jnp.float32) * D_FF ** -0.5
    return {"x_prompt": x_prompt, "x_sample": x_sample, "state_ret": state_ret,
            "cache_fox_k": cache_fox_k, "cache_fox_v": cache_fox_v, "cache_fox_logf": cache_fox_logf,
            "meta_tokens": meta_tokens, "norm1_g": norm1_g, "w_in": w_in, "b_forget": b_forget,
            "q_norm_g": q_norm_g, "k_norm_g": k_norm_g, "ret_gn_g": ret_gn_g,
            "w_ret_out": w_ret_out, "w_fox_out": w_fox_out, "w_o": w_o, "norm2_g": norm2_g,
            "w_ff1": w_ff1, "w_ff2": w_ff2}


def reference(x_prompt, x_sample, state_ret, cache_fox_k, cache_fox_v, cache_fox_logf,
              meta_tokens, norm1_g, w_in, b_forget, q_norm_g, k_norm_g, ret_gn_g,
              w_ret_out, w_fox_out, w_o, norm2_g, w_ff1, w_ff2):
    bp = x_prompt.shape[0]
    dt = x_prompt.dtype
    hp = jnp.concatenate([jnp.zeros((bp, META_PAD, D_MODEL), dt),
                          jnp.broadcast_to(meta_tokens.astype(dt)[None], (bp, N_META, D_MODEL)),
                          x_prompt], axis=1)
    pos_p = jnp.arange(hp.shape[1]) - (META_PAD + N_META)
    valid_p = pos_p >= -N_META
    hs = x_sample
    p_st, p_k, p_v, p_lf, s_st, s_k, s_v, s_lf = [], [], [], [], [], [], [], []
    for l in range(DEPTH):
        lw = (norm1_g[l], w_in[l], b_forget[l], q_norm_g[l], k_norm_g[l], ret_gn_g[l],
              w_ret_out[l], w_fox_out[l], w_o[l], norm2_g[l], w_ff1[l], w_ff2[l])
        hp, st, kr, vr, lf = prompt_layer(hp, pos_p, valid_p, lw)
        p_st.append(st)
        p_k.append(kr)
        p_v.append(vr)
        p_lf.append(lf)
        hs, st, kr, vr, lf = sample_layer(hs, state_ret[l], cache_fox_k[l], cache_fox_v[l], cache_fox_logf[l], lw)
        s_st.append(st)
        s_k.append(kr)
        s_v.append(vr)
        s_lf.append(lf)
    y_prompt = hp[:, META_PAD + N_META:]
    return (y_prompt, hs, jnp.stack(p_st), jnp.stack(p_k), jnp.stack(p_v), jnp.stack(p_lf),
            jnp.stack(s_st), jnp.stack(s_k), jnp.stack(s_v), jnp.stack(s_lf))
```

```python
import functools
import math

import jax
import jax.numpy as jnp
from jax import lax
from jax.experimental import pallas as pl
from jax.experimental.pallas import tpu as pltpu

F32 = jnp.float32
BF16 = jnp.bfloat16

D_MODEL = 1024
N_META = 16
RET_HEADS = 4
RET_QK_DIM = 128
RET_V_DIM = 256
FOX_HEADS = 8
FOX_DIM = 128
D_FF = 4 * D_MODEL
ROPE_BASE = 10000.0
EPS = 1e-6
NEG_INF = -1e30
LANES = 128
VMEM_LIMIT = 56 * 1024 * 1024

_RQ, _RK, _RV, _RG, _FQ, _FK, _FV, _GA, _GB, _FF, _WTOT = (
    0, 512, 1024, 2048, 3072, 4096, 5120, 6144, 7168, 8192, 8320)


def _cparams(sem):
    return pltpu.CompilerParams(dimension_semantics=sem, vmem_limit_bytes=VMEM_LIMIT)


def _resident(shape):
    nd = len(shape)
    return pl.BlockSpec(shape, lambda *_: (0,) * nd, pipeline_mode=pl.Buffered(1))


def _in_proj_kernel(x_ref, cos_ref, sin_ref, g1_ref, w_ref, qg_ref, kg_ref, bf_ref,
                    rq_ref, rk_ref, rv_ref, rg_ref, fq_ref, fk_ref, fv_ref, ga_ref, gb_ref, lf_ref):
    x = x_ref[...]
    ms = jnp.mean(x * x, axis=-1, keepdims=True)
    xn = (x * lax.rsqrt(ms + EPS) * g1_ref[...]).astype(BF16)
    cosd = cos_ref[...]
    sind = sin_ref[...]

    def proj(lo, hi):
        return jnp.dot(xn, w_ref[:, lo:hi], preferred_element_type=F32)

    def rope_store(z, out_ref, scale):
        for hh in range(RET_HEADS):
            zh = z[:, hh * RET_QK_DIM:(hh + 1) * RET_QK_DIM]
            r = zh * cosd + pltpu.roll(zh, RET_QK_DIM // 2, 1) * sind
            if scale is not None:
                r = r * scale
            out_ref[:, hh * RET_QK_DIM:(hh + 1) * RET_QK_DIM] = r.astype(out_ref.dtype)

    def headnorm_store(z, g, out_ref, scale):
        for hh in range(FOX_HEADS):
            zh = z[:, hh * FOX_DIM:(hh + 1) * FOX_DIM]
            y = zh * lax.rsqrt(jnp.mean(zh * zh, axis=-1, keepdims=True) + EPS) * g
            if scale is not None:
                y = y * scale
            out_ref[:, hh * FOX_DIM:(hh + 1) * FOX_DIM] = y.astype(out_ref.dtype)

    rope_store(proj(_RQ, _RK), rq_ref, None)
    rope_store(proj(_RK, _RV), rk_ref, RET_QK_DIM ** -0.5)
    rv_ref[...] = proj(_RV, _RG).astype(rv_ref.dtype)
    rg_ref[...] = proj(_RG, _FQ).astype(rg_ref.dtype)
    headnorm_store(proj(_FQ, _FK), qg_ref[...], fq_ref, FOX_DIM ** -0.5)
    headnorm_store(proj(_FK, _FV), kg_ref[...], fk_ref, None)
    fv_ref[...] = proj(_FV, _GA)
    ga_ref[...] = proj(_GA, _GB).astype(ga_ref.dtype)
    gb_ref[...] = proj(_GB, _FF).astype(gb_ref.dtype)
    v = proj(_FF, _WTOT) + bf_ref[...]
    lf = jnp.minimum(v, 0.0) - jnp.log1p(jnp.exp(-jnp.abs(v)))
    lf_ref[...] = lf[:, :FOX_HEADS]


def _in_proj(x, cosd, sind, g1, w_all, qg, kg, bfp, tm):
    bx, t, _ = x.shape
    grid = (bx, t // tm)

    def tok(width):
        return pl.BlockSpec((None, tm, width), lambda b, i: (b, i, 0))

    def out(width, dt):
        return jax.ShapeDtypeStruct((bx, t, width), dt)

    return pl.pallas_call(
        _in_proj_kernel,
        grid=grid,
        in_specs=[tok(D_MODEL),
                  pl.BlockSpec((tm, LANES), lambda b, i: (i, 0)),
                  pl.BlockSpec((tm, LANES), lambda b, i: (i, 0)),
                  _resident((1, D_MODEL)), _resident((D_MODEL, _WTOT)),
                  _resident((1, LANES)), _resident((1, LANES)), _resident((1, LANES))],
        out_specs=[tok(512), tok(512), tok(1024), tok(1024), tok(1024), tok(1024), tok(1024),
                   tok(1024), tok(1024), tok(FOX_HEADS)],
        out_shape=[out(512, BF16), out(512, BF16), out(1024, BF16), out(1024, BF16),
                   out(1024, BF16), out(1024, F32), out(1024, F32), out(1024, BF16),
                   out(1024, BF16), out(FOX_HEADS, F32)],
        compiler_params=_cparams(("parallel", "parallel")),
        name="in_proj",
    )(x, cosd, sind, g1, w_all, qg, kg, bfp)


def _ret_kernel(q_ref, k_ref, v_ref, g_ref, gn_ref, s0_ref, o_ref, sout_ref,
                st_ref, d_ref, qd_ref, kd_ref, *, chunk, n_chunks, valid_len):
    t = pl.program_id(1)
    log_gamma = [math.log(1.0 - 2.0 ** (-5.0 - h)) for h in range(RET_HEADS)]

    @pl.when(t == 0)
    def _init():
        st_ref[...] = s0_ref[...].astype(F32)
        n = lax.broadcasted_iota(jnp.int32, (chunk, chunk), 0)
        m = lax.broadcasted_iota(jnp.int32, (chunk, chunk), 1)
        diff = (n - m).astype(F32)
        nn = lax.broadcasted_iota(jnp.int32, (chunk, LANES), 0).astype(F32)
        for h in range(RET_HEADS):
            lg = log_gamma[h]
            d_ref[h] = jnp.where(diff >= 0, jnp.exp(lg * jnp.maximum(diff, 0.0)), 0.0)
            qd_ref[h] = jnp.exp(lg * (nn + 1.0))
            kd_ref[h] = jnp.exp(lg * (valid_len - 1.0 - nn))

    def chunk_body(cc, carry):
        r0 = pl.multiple_of(cc * chunk, chunk)
        rows = pl.ds(r0, chunk)
        for h in range(RET_HEADS):
            qk = slice(h * RET_QK_DIM, (h + 1) * RET_QK_DIM)
            vv = slice(h * RET_V_DIM, (h + 1) * RET_V_DIM)
            q = q_ref[rows, qk]
            k = k_ref[rows, qk]
            v = v_ref[rows, vv]
            s = lax.dot_general(q, k, (((1,), (1,)), ((), ())), preferred_element_type=F32)
            s = s * d_ref[h]
            inner = jnp.dot(s.astype(BF16), v, preferred_element_type=F32)
            state = st_ref[h]
            qs = (q.astype(F32) * qd_ref[h]).astype(BF16)
            cross = jnp.dot(qs, state.astype(BF16), preferred_element_type=F32)
            o = inner + cross
            ks = (k.astype(F32) * kd_ref[h]).astype(BF16)
            upd = lax.dot_general(ks, v, (((0,), (0,)), ((), ())), preferred_element_type=F32)
            st_ref[h] = math.exp(log_gamma[h] * valid_len) * state + upd
            mu = jnp.mean(o, axis=-1, keepdims=True)
            oc = o - mu
            yn = oc * lax.rsqrt(jnp.mean(oc * oc, axis=-1, keepdims=True) + EPS) * gn_ref[:, vv]
            g = g_ref[rows, vv].astype(F32)
            o_ref[rows, vv] = (g * jax.nn.sigmoid(g) * yn).astype(o_ref.dtype)
        return carry

    lax.fori_loop(0, n_chunks, chunk_body, 0)

    @pl.when(t == pl.num_programs(1) - 1)
    def _fin():
        sout_ref[...] = st_ref[...]


def _retention(rq, rk, rv, rg, gn_g, s0, *, tb, chunk, valid_len=None, shared_state=False):
    bx, t, _ = rq.shape
    if valid_len is None:
        valid_len = chunk
    else:
        assert t == chunk
    grid = (bx, t // tb)

    def tok(width):
        return pl.BlockSpec((None, tb, width), lambda b, i: (b, i, 0))

    st_shape = (RET_HEADS, RET_QK_DIM, RET_V_DIM)
    s0_map = (lambda b, i: (0, 0, 0, 0)) if shared_state else (lambda b, i: (b, 0, 0, 0))
    kern = functools.partial(_ret_kernel, chunk=chunk, n_chunks=tb // chunk, valid_len=float(valid_len))
    return pl.pallas_call(
        kern,
        grid=grid,
        in_specs=[tok(512), tok(512), tok(1024), tok(1024), _resident((1, 1024)),
                  pl.BlockSpec((None,) + st_shape, s0_map)],
        out_specs=[tok(1024), pl.BlockSpec((None,) + st_shape, lambda b, i: (b, 0, 0, 0))],
        out_shape=[jax.ShapeDtypeStruct((bx, t, 1024), BF16),
                   jax.ShapeDtypeStruct((bx,) + st_shape, F32)],
        scratch_shapes=[pltpu.VMEM(st_shape, F32),
                        pltpu.VMEM((RET_HEADS, chunk, chunk), F32),
                        pltpu.VMEM((RET_HEADS, chunk, LANES), F32),
                        pltpu.VMEM((RET_HEADS, chunk, LANES), F32)],
        compiler_params=_cparams(("parallel", "arbitrary")),
        name="retention",
    )(rq, rk, rv, rg, gn_g, s0)


def _cumsum_kernel(x_ref, o_ref):
    x = x_ref[...]
    n = x.shape[-1]
    lane = lax.broadcasted_iota(jnp.int32, x.shape, 1)
    s = 1
    while s < n:
        x = x + jnp.where(lane >= s, pltpu.roll(x, s, 1), 0.0)
        s *= 2
    o_ref[...] = x


def _cumsum_lanes(x):
    bx, hh, n = x.shape
    return pl.pallas_call(
        _cumsum_kernel,
        grid=(bx,),
        in_specs=[pl.BlockSpec((None, hh, n), lambda b: (b, 0, 0))],
        out_specs=pl.BlockSpec((None, hh, n), lambda b: (b, 0, 0)),
        out_shape=jax.ShapeDtypeStruct((bx, hh, n), F32),
        compiler_params=_cparams(("parallel",)),
        name="cumsum",
    )(x)


def _online_update(t, cq, v, m_ref, l_ref, acc_ref):
    m_old = m_ref[...]
    m_new = jnp.maximum(m_old, jnp.max(t, axis=-1, keepdims=True) + cq)
    alpha = jnp.exp(m_old - m_new)
    p = jnp.exp(t - (m_new - cq))
    l_ref[...] = alpha * l_ref[...] + jnp.sum(p, axis=-1, keepdims=True)
    acc_ref[...] = alpha * acc_ref[...] + jnp.dot(p.astype(BF16), v, preferred_element_type=F32)
    m_ref[...] = m_new


def _fox_prompt_kernel(q_ref, k_ref, v_ref, ckm_ref, ckf_ref, cq_ref, o_ref,
                       kb_ref, vb_ref, m_ref, l_ref, acc_ref, *, tq):
    h = pl.program_id(1)
    i = pl.program_id(2)

    @pl.when(i == 0)
    def _cast():
        kb_ref[...] = k_ref[...].astype(BF16)
        vb_ref[...] = v_ref[...].astype(BF16)

    q = q_ref[...]
    cqb = cq_ref[...]
    hsel = lax.broadcasted_iota(jnp.int32, cqb.shape, 1) == h
    cq = jnp.sum(jnp.where(hsel, cqb, 0.0), axis=-1, keepdims=True)

    m_ref[...] = jnp.full(m_ref.shape, NEG_INF, F32)
    l_ref[...] = jnp.zeros(l_ref.shape, F32)
    acc_ref[...] = jnp.zeros(acc_ref.shape, F32)

    def scores(k):
        return lax.dot_general(q, k, (((1,), (1,)), ((), ())), preferred_element_type=F32)

    t = scores(kb_ref[0:LANES, :]) - ckm_ref[...]
    lane = lax.broadcasted_iota(jnp.int32, t.shape, 1)
    t = jnp.where(lane < N_META, t, NEG_INF)
    _online_update(t, cq, vb_ref[0:LANES, :], m_ref, l_ref, acc_ref)

    def full_tile(j, carry):
        c0 = pl.multiple_of(j * tq, tq)
        r0 = pl.multiple_of(j * tq + N_META, 16)
        t = scores(kb_ref[pl.ds(r0, tq), :]) - ckf_ref[:, pl.ds(c0, tq)]
        _online_update(t, cq, vb_ref[pl.ds(r0, tq), :], m_ref, l_ref, acc_ref)
        return carry

    lax.fori_loop(0, i, full_tile, 0)

    c0 = pl.multiple_of(i * tq, tq)
    r0 = pl.multiple_of(i * tq + N_META, 16)
    t = scores(kb_ref[pl.ds(r0, tq), :]) - ckf_ref[:, pl.ds(c0, tq)]
    qi = lax.broadcasted_iota(jnp.int32, t.shape, 0)
    ki = lax.broadcasted_iota(jnp.int32, t.shape, 1)
    t = jnp.where(ki <= qi, t, NEG_INF)
    _online_update(t, cq, vb_ref[pl.ds(r0, tq), :], m_ref, l_ref, acc_ref)

    o_ref[...] = (acc_ref[...] / l_ref[...]).astype(o_ref.dtype)


def _fox_prompt(fq, k_all, v_all, ck_meta, ck_frames, cq_tok, *, tq):
    bx, s, _ = fq.shape
    lk = k_all.shape[1]
    grid = (bx, FOX_HEADS, s // tq)
    kern = functools.partial(_fox_prompt_kernel, tq=tq)
    return pl.pallas_call(
        kern,
        grid=grid,
        in_specs=[pl.BlockSpec((None, tq, FOX_DIM), lambda b, h, i: (b, i, h)),
                  pl.BlockSpec((None, lk, FOX_DIM), lambda b, h, i: (b, 0, h)),
                  pl.BlockSpec((None, lk, FOX_DIM), lambda b, h, i: (b, 0, h)),
                  pl.BlockSpec((None, None, 1, LANES), lambda b, h, i: (b, h, 0, 0)),
                  pl.BlockSpec((None, None, 1, s), lambda b, h, i: (b, h, 0, 0)),
                  pl.BlockSpec((None, tq, FOX_HEADS), lambda b, h, i: (b, i, 0))],
        out_specs=pl.BlockSpec((None, tq, FOX_DIM), lambda b, h, i: (b, i, h)),
        out_shape=jax.ShapeDtypeStruct((bx, s, FOX_HEADS * FOX_DIM), BF16),
        scratch_shapes=[pltpu.VMEM((lk, FOX_DIM), BF16), pltpu.VMEM((lk, FOX_DIM), BF16),
                        pltpu.VMEM((tq, 1), F32), pltpu.VMEM((tq, 1), F32),
                        pltpu.VMEM((tq, FOX_DIM), F32)],
        compiler_params=_cparams(("parallel", "parallel", "arbitrary")),
        name="fox_prompt",
    )(fq, k_all, v_all, ck_meta, ck_frames, cq_tok)


def _fox_sample_kernel(q_ref, kc_ref, vc_ref, kn_ref, vn_ref, ckc_ref, ckn_ref, cq_ref, o_ref,
                       m_ref, l_ref, acc_ref, *, n_cache_tiles):
    j = pl.program_id(1)

    @pl.when(j == 0)
    def _init():
        m_ref[...] = jnp.full(m_ref.shape, NEG_INF, F32)
        l_ref[...] = jnp.zeros(l_ref.shape, F32)
        acc_ref[...] = jnp.zeros(acc_ref.shape, F32)

    def step(k_ref, v_ref, ck_ref, causal):
        for h in range(FOX_HEADS):
            cols = slice(h * FOX_DIM, (h + 1) * FOX_DIM)
            q = q_ref[:, cols]
            k = k_ref[:, cols].astype(BF16)
            v = v_ref[:, cols].astype(BF16)
            t = lax.dot_general(q, k, (((1,), (1,)), ((), ())), preferred_element_type=F32)
            t = t - ck_ref[h:h + 1, :]
            if causal:
                qi = lax.broadcasted_iota(jnp.int32, t.shape, 0)
                ki = lax.broadcasted_iota(jnp.int32, t.shape, 1)
                t = jnp.where(ki <= qi, t, NEG_INF)
            _online_update(t, cq_ref[:, h:h + 1], v, m_ref.at[h], l_ref.at[h], acc_ref.at[h])

    @pl.when(j < n_cache_tiles)
    def _cache():
        step(kc_ref, vc_ref, ckc_ref, False)

    @pl.when(j == n_cache_tiles)
    def _new():
        step(kn_ref, vn_ref, ckn_ref, True)
        for h in range(FOX_HEADS):
            cols = slice(h * FOX_DIM, (h + 1) * FOX_DIM)
            o_ref[:, cols] = (acc_ref[h] / l_ref[h]).astype(o_ref.dtype)


def _fox_sample(fq, k_cache, v_cache, k_new, v_new, ck_cache, ck_new, cq_tok, *, tk):
    bx, t, _ = fq.shape
    p = k_cache.shape[1]
    n_tiles = p // tk
    grid = (bx, n_tiles + 1)
    last = n_tiles - 1
    kern = functools.partial(_fox_sample_kernel, n_cache_tiles=n_tiles)
    return pl.pallas_call(
        kern,
        grid=grid,
        in_specs=[pl.BlockSpec((None, t, 1024), lambda b, j: (b, 0, 0)),
                  pl.BlockSpec((None, tk, 1024), lambda b, j: (b, jnp.minimum(j, last), 0)),
                  pl.BlockSpec((None, tk, 1024), lambda b, j: (b, jnp.minimum(j, last), 0)),
                  pl.BlockSpec((None, t, 1024), lambda b, j: (b, 0, 0)),
                  pl.BlockSpec((None, t, 1024), lambda b, j: (b, 0, 0)),
                  pl.BlockSpec((None, FOX_HEADS, tk), lambda b, j: (b, 0, jnp.minimum(j, last))),
                  pl.BlockSpec((None, FOX_HEADS, t), lambda b, j: (b, 0, 0)),
                  pl.BlockSpec((None, t, FOX_HEADS), lambda b, j: (b, 0, 0))],
        out_specs=pl.BlockSpec((None, t, 1024), lambda b, j: (b, 0, 0)),
        out_shape=jax.ShapeDtypeStruct((bx, t, 1024), BF16),
        scratch_shapes=[pltpu.VMEM((FOX_HEADS, t, 1), F32), pltpu.VMEM((FOX_HEADS, t, 1), F32),
                        pltpu.VMEM((FOX_HEADS, t, FOX_DIM), F32)],
        compiler_params=_cparams(("parallel", "arbitrary")),
        name="fox_sample",
    )(fq, k_cache, v_cache, k_new, v_new, ck_cache, ck_new, cq_tok)


def _merge_mlp_kernel(h_ref, ro_ref, fo_ref, ga_ref, gb_ref, wr_ref, wf_ref, wo_ref, g2_ref,
                      w1_ref, w2_ref, y_ref, *, ff_chunk):
    a = jnp.dot(ro_ref[...], wr_ref[...], preferred_element_type=F32)
    b = jnp.dot(fo_ref[...], wf_ref[...], preferred_element_type=F32)
    m = jax.nn.sigmoid(ga_ref[...].astype(F32)) * a + jax.nn.sigmoid(gb_ref[...].astype(F32)) * b
    h2 = h_ref[...] + jnp.dot(m.astype(BF16), wo_ref[...], preferred_element_type=F32)
    ms = jnp.mean(h2 * h2, axis=-1, keepdims=True)
    xn = (h2 * lax.rsqrt(ms + EPS) * g2_ref[...]).astype(BF16)
    acc = h2
    for c in range(D_FF // ff_chunk):
        cols = slice(c * ff_chunk, (c + 1) * ff_chunk)
        u = jnp.maximum(jnp.dot(xn, w1_ref[:, cols], preferred_element_type=F32), 0.0)
        acc = acc + jnp.dot((u * u).astype(BF16), w2_ref[cols, :], preferred_element_type=F32)
    y_ref[...] = acc


def _merge_mlp(h, ret_o, fox_o, ga, gb, wr, wf, wo, g2, w1, w2, *, tm):
    bx, t, _ = h.shape
    grid = (bx, t // tm)

    def tok():
        return pl.BlockSpec((None, tm, D_MODEL), lambda b, i: (b, i, 0))

    kern = functools.partial(_merge_mlp_kernel, ff_chunk=1024)
    return pl.pallas_call(
        kern,
        grid=grid,
        in_specs=[tok(), tok(), tok(), tok(), tok(),
                  _resident((1024, D_MODEL)), _resident((1024, D_MODEL)),
                  _resident((D_MODEL, D_MODEL)), _resident((1, D_MODEL)),
                  _resident((D_MODEL, D_FF)), _resident((D_FF, D_MODEL))],
        out_specs=tok(),
        out_shape=jax.ShapeDtypeStruct((bx, t, D_MODEL), F32),
        compiler_params=_cparams(("parallel", "parallel")),
        name="merge_mlp",
    )(h, ret_o, fox_o, ga, gb, wr, wf, wo, g2, w1, w2)


def _rope_tables(pos):
    half = RET_QK_DIM // 2
    inv = ROPE_BASE ** (-jnp.arange(half, dtype=F32) / half)
    ang = pos.astype(F32)[:, None] * inv[None, :]
    cos, sin = jnp.cos(ang), jnp.sin(ang)
    return jnp.concatenate([cos, cos], axis=-1), jnp.concatenate([-sin, sin], axis=-1)


def _pad_tokens(x, t_pad):
    return jnp.pad(x, ((0, 0), (0, t_pad - x.shape[1]), (0, 0)))


def _pad_lanes(x, n):
    return jnp.pad(x, [(0, 0)] * (x.ndim - 1) + [(0, n - x.shape[-1])])


def _tile(n, pref):
    while n % pref:
        pref //= 2
    return pref


def kernel(x_prompt, x_sample, state_ret, cache_fox_k, cache_fox_v, cache_fox_logf, meta_tokens,
           norm1_g, w_in, b_forget, q_norm_g, k_norm_g, ret_gn_g, w_ret_out, w_fox_out, w_o,
           norm2_g, w_ff1, w_ff2):
    bp, seq, _ = x_prompt.shape
    bs, dec, _ = x_sample.shape
    past = cache_fox_k.shape[2]
    tpad = LANES
    assert dec <= tpad and N_META <= tpad and norm1_g.shape[0] == 1

    w = w_in[0]
    w_all = jnp.concatenate(
        [w[:, :6144], w[:, 6152:8200], _pad_lanes(w[:, 6144:6152], LANES)], axis=1).astype(BF16)
    g1 = norm1_g[0][None]
    qg = q_norm_g[0][None]
    kg = k_norm_g[0][None]
    bfp = _pad_lanes(b_forget[0][None], LANES)
    gn = ret_gn_g[0][None]
    g2 = norm2_g[0][None]
    wr, wf, wo = w_ret_out[0].astype(BF16), w_fox_out[0].astype(BF16), w_o[0].astype(BF16)
    w1, w2 = w_ff1[0].astype(BF16), w_ff2[0].astype(BF16)
    proj = functools.partial(_in_proj, g1=g1, w_all=w_all, qg=qg, kg=kg, bfp=bfp)

    x_meta = _pad_tokens(meta_tokens[None], tpad)
    x_samp = _pad_tokens(x_sample, tpad)
    cos_m, sin_m = _rope_tables(jnp.arange(tpad) - N_META)
    cos_p, sin_p = _rope_tables(jnp.arange(seq))
    cos_s, sin_s = _rope_tables(past + jnp.arange(tpad))
    tm_p = _tile(seq, 512)
    m_rq, m_rk, m_rv, m_rg, _, m_fk, m_fv, _, _, m_lf = proj(x_meta, cos_m, sin_m, tm=tpad)
    p_rq, p_rk, p_rv, p_rg, p_fq, p_fk, p_fv, p_ga, p_gb, p_lf = proj(x_prompt, cos_p, sin_p, tm=tm_p)
    s_rq, s_rk, s_rv, s_rg, s_fq, s_fk, s_fv, s_ga, s_gb, s_lf = proj(x_samp, cos_s, sin_s, tm=tpad)

    zero_state = jnp.zeros((1, RET_HEADS, RET_QK_DIM, RET_V_DIM), F32)
    _, meta_state = _retention(m_rq, m_rk, m_rv, m_rg, gn, zero_state,
                               tb=tpad, chunk=tpad, valid_len=N_META)
    p_ret, p_state = _retention(p_rq, p_rk, p_rv, p_rg, gn, meta_state,
                                tb=_tile(seq, 1024), chunk=_tile(seq, 256), shared_state=True)
    s_ret, s_state = _retention(s_rq, s_rk, s_rv, s_rg, gn, state_ret[0],
                                tb=tpad, chunk=tpad, valid_len=dec)

    lk = N_META + seq
    k_all = jnp.concatenate([jnp.broadcast_to(m_fk[:, :N_META], (bp, N_META, 1024)), p_fk], axis=1)
    v_all = jnp.concatenate([jnp.broadcast_to(m_fv[:, :N_META], (bp, N_META, 1024)), p_fv], axis=1)
    lf_all = jnp.concatenate([jnp.broadcast_to(m_lf[:, :N_META], (bp, N_META, FOX_HEADS)), p_lf], axis=1)
    lkp = -(-lk // LANES) * LANES
    c_hm = _cumsum_lanes(_pad_lanes(lf_all.transpose(0, 2, 1), lkp))
    ck_meta = c_hm[:, :, None, :LANES]
    ck_frames = c_hm[:, :, None, N_META:lk]
    cq_tok = c_hm[:, :, N_META:lk].transpose(0, 2, 1)
    p_fox = _fox_prompt(p_fq, k_all, v_all, ck_meta, ck_frames, cq_tok, tq=_tile(seq, 512))

    kc = cache_fox_k[0].reshape(bs, past, 1024)
    vc = cache_fox_v[0].reshape(bs, past, 1024)
    lf_s = jnp.concatenate([cache_fox_logf[0].astype(F32), s_lf], axis=1)
    cs_hm = _cumsum_lanes(lf_s.transpose(0, 2, 1))
    s_fox = _fox_sample(s_fq, kc, vc, s_fk, s_fv, cs_hm[:, :, :past], cs_hm[:, :, past:],
                        cs_hm[:, :, past:].transpose(0, 2, 1), tk=_tile(past, 512))

    y_p = _merge_mlp(x_prompt, p_ret, p_fox, p_ga, p_gb, wr, wf, wo, g2, w1, w2, tm=tm_p)
    y_s = _merge_mlp(x_samp, s_ret, s_fox, s_ga, s_gb, wr, wf, wo, g2, w1, w2, tm=tpad)

    hd = (FOX_HEADS, FOX_DIM)
    return (y_p, y_s[:, :dec], p_state[None],
            k_all.reshape(1, bp, lk, *hd), v_all.reshape(1, bp, lk, *hd), lf_all[None],
            s_state[None].astype(state_ret.dtype),
            s_fk[:, :dec].reshape(1, bs, dec, *hd), s_fv[:, :dec].reshape(1, bs, dec, *hd),
            s_lf[None, :, :dec])
```
